```python
import jax, jax.numpy as jnp
from jax import lax
import numpy as np

D_MODEL = 1024
BATCH = 4
SEQ = 4096
DEPTH = 1
DEC_BATCH = 128
DEC_SEQ = 4
PAST_LEN = 16384
PAGE_SIZE = 128

GLA_WIDTH = D_MODEL // 2
SWA_WIDTH = D_MODEL - GLA_WIDTH
GLA_HEADS = 4
GLA_DV = GLA_WIDTH // GLA_HEADS
GLA_DK = GLA_DV // 2
GLA_QK = GLA_HEADS * GLA_DK
GLA_GATE_RANK = 16
GLA_GATE_NORM = 16.0
GLA_CHUNK = 64
HEAD_DIM = 64
SWA_HEADS = SWA_WIDTH // HEAD_DIM
SWA_KV_HEADS = 2
SWA_GROUP = SWA_HEADS // SWA_KV_HEADS
WINDOW = 128
ROPE_DIM = HEAD_DIM // 4
ROPE_THETA = 500000.0
N_EXPERTS = 32
TOP_K = 4
EXPERT_DFF = D_MODEL
MOE_BLOCK = 128
SWIGLU_LIMIT = 7.0
SWIGLU_ALPHA = 1.702
NORM_EPS = 1e-5
DEEPNORM_ALPHA = (2 * DEPTH) ** 0.25
DEEPNORM_BETA = (8 * DEPTH) ** -0.25
IN_SPLITS = (GLA_QK, GLA_QK, GLA_WIDTH, GLA_WIDTH, GLA_GATE_RANK,
             SWA_HEADS * HEAD_DIM, SWA_KV_HEADS * HEAD_DIM, SWA_KV_HEADS * HEAD_DIM)
D_IN = sum(IN_SPLITS)

kernel_name = 'hymba_gla_swa_sink_moe_deepnorm_step'


def layer_norm(x, g, b):
    xf = x.astype(jnp.float32)
    mu = xf.mean(-1, keepdims=True)
    var = jnp.square(xf - mu).mean(-1, keepdims=True)
    return ((xf - mu) * lax.rsqrt(var + NORM_EPS) * g.astype(jnp.float32)
            + b.astype(jnp.float32)).astype(x.dtype)


def rms_norm_f32(x, g):
    xf = x.astype(jnp.float32)
    return xf * lax.rsqrt(jnp.mean(xf * xf, -1, keepdims=True) + NORM_EPS) * g.astype(jnp.float32)


def rope_partial(x, pos):
    half = ROPE_DIM // 2
    freqs = ROPE_THETA ** (-jnp.arange(half, dtype=jnp.float32) * 2.0 / ROPE_DIM)
    ang = pos.astype(jnp.float32)[:, None] * freqs[None, :]
    cos = jnp.cos(ang)[:, None, :].astype(x.dtype)
    sin = jnp.sin(ang)[:, None, :].astype(x.dtype)
    x1 = x[..., :half]
    x2 = x[..., half:ROPE_DIM]
    return jnp.concatenate([x1 * cos - x2 * sin, x2 * cos + x1 * sin, x[..., ROPE_DIM:]], axis=-1)


def gla_recurrence(q, k, v, gk, s0):
    B, L, H, _ = q.shape
    C = min(GLA_CHUNK, L)
    n = -(-L // C)
    pad = n * C - L

    def blocks(t):
        t = jnp.pad(t.astype(jnp.float32), ((0, 0), (0, pad), (0, 0), (0, 0)))
        return t.reshape(B, n, C, H, t.shape[-1]).transpose(1, 0, 3, 2, 4)

    causal = jnp.tril(jnp.ones((C, C), dtype=bool))

    def step(S, inp):
        qi, ki, vi, gi = inp
        b = jnp.cumsum(gi, axis=2)
        b_last = b[:, :, -1:, :]
        o_inter = jnp.einsum('bhtd,bhde->bhte', qi * jnp.exp(b), S)
        diff = jnp.where(causal[:, :, None], b[:, :, :, None, :] - b[:, :, None, :, :], -jnp.inf)
        att = jnp.einsum('bhtd,bhtsd,bhsd->bhts', qi, jnp.exp(diff), ki)
        o = o_inter + jnp.einsum('bhts,bhse->bhte', att, vi)
        S = S * jnp.exp(b_last[:, :, 0, :, None]) + jnp.einsum(
            'bhsd,bhse->bhde', ki * jnp.exp(b_last - b), vi)
        return S, o

    S, o = lax.scan(step, s0.astype(jnp.float32), (blocks(q), blocks(k), blocks(v), blocks(gk)))
    o = o.transpose(1, 0, 3, 2, 4).reshape(B, n * C, H, v.shape[-1])[:, :L]
    return o, S


def sink_attention(q, k, v, mask, sinks):
    s = jnp.einsum('bnqhgd,bnkhd->bnhgqk', q, k).astype(jnp.float32) * (HEAD_DIM ** -0.5)
    s = jnp.where(mask[None, :, None, None], s, -jnp.inf)
    sink = sinks.astype(jnp.float32)[None, None, :, :, None, None]
    m = jnp.maximum(s.max(-1, keepdims=True), sink)
    p = jnp.exp(s - m)
    p = (p / (p.sum(-1, keepdims=True) + jnp.exp(sink - m))).astype(v.dtype)
    return jnp.einsum('bnhgqk,bnkhd->bnqhgd', p, v)


def swa_prompt(q, k, v, sinks):
    B, S = q.shape[:2]
    n = S // WINDOW
    qb = q.reshape(B, n, WINDOW, SWA_KV_HEADS, SWA_GROUP, HEAD_DIM)

    def band(t):
        tp = jnp.pad(t, ((0, 0), (WINDOW, 0), (0, 0), (0, 0))).reshape(B, n + 1, WINDOW, SWA_KV_HEADS, HEAD_DIM)
        return jnp.concatenate([tp[:, :-1], tp[:, 1:]], axis=2)

    qpos = jnp.arange(S).reshape(n, WINDOW)
    kpos = jnp.arange(n)[:, None] * WINDOW - WINDOW + jnp.arange(2 * WINDOW)[None, :]
    d = qpos[:, :, None] - kpos[:, None, :]
    mask = (d >= 0) & (d < WINDOW) & (kpos[:, None, :] >= 0)
    o = sink_attention(qb, band(k), band(v), mask, sinks)
    return o.reshape(B, S, SWA_WIDTH), k[:, -WINDOW:], v[:, -WINDOW:]


def swa_sample(q, k, v, k_buf, v_buf, pos, sinks):
    B, L = q.shape[:2]
    W = k_buf.shape[1]
    kk = jnp.concatenate([k_buf.astype(k.dtype), k], axis=1)
    vv = jnp.concatenate([v_buf.astype(v.dtype), v], axis=1)
    kpos = jnp.concatenate([pos[0] - W + jnp.arange(W), pos])
    d = pos[:, None] - kpos[None, :]
    mask = ((d >= 0) & (d < WINDOW) & (kpos[None, :] >= 0))[None]
    o = sink_attention(q.reshape(B, 1, L, SWA_KV_HEADS, SWA_GROUP, HEAD_DIM),
                       kk[:, None], vv[:, None], mask, sinks)
    return o.reshape(B, L, SWA_WIDTH), kk[:, -W:], vv[:, -W:]


def clamped_swiglu(gate, up):
    gate = jnp.minimum(gate, SWIGLU_LIMIT)
    up = jnp.clip(up, -SWIGLU_LIMIT, SWIGLU_LIMIT)
    return gate * jax.nn.sigmoid(SWIGLU_ALPHA * gate) * (up + 1.0)


def moe_ffn(x, w_router, b_router, w_gate, b_gate, w_up, b_up, w_down, b_down):
    shp = x.shape
    xt = x.reshape(-1, D_MODEL)
    T = xt.shape[0]
    logits = (xt @ w_router + b_router).astype(jnp.float32)
    top_val, top_idx = lax.top_k(logits, TOP_K)
    gate_w = jax.nn.softmax(top_val, axis=-1)
    A = T * TOP_K
    flat_e = top_idx.reshape(-1)
    order = jnp.argsort(flat_e)
    sorted_e = flat_e[order]
    token_of = order // TOP_K
    counts = jnp.bincount(flat_e, length=N_EXPERTS)
    padded = (counts + MOE_BLOCK - 1) // MOE_BLOCK * MOE_BLOCK
    pad_end = jnp.cumsum(padded)
    pad_start = pad_end - padded
    start = jnp.cumsum(counts) - counts
    dest = pad_start[sorted_e] + jnp.arange(A) - start[sorted_e]
    n_blocks = -(-A // MOE_BLOCK) + N_EXPERTS
    slot_token = jnp.full((n_blocks * MOE_BLOCK,), T, jnp.int32).at[dest].set(token_of)
    block_expert = jnp.minimum(
        jnp.searchsorted(pad_end, jnp.arange(n_blocks) * MOE_BLOCK, side='right'), N_EXPERTS - 1)
    x_pad = jnp.concatenate([xt, jnp.zeros((1, D_MODEL), xt.dtype)], axis=0)
    xb = x_pad[slot_token].reshape(n_blocks, MOE_BLOCK, D_MODEL)

    def expert_block(args):
        xe, e = args
        h = clamped_swiglu(xe @ w_gate[e] + b_gate[e], xe @ w_up[e] + b_up[e])
        return h @ w_down[e] + b_down[e]

    yb = lax.map(expert_block, (xb, block_expert)).reshape(n_blocks * MOE_BLOCK, D_MODEL)
    y_assign = yb[dest] * gate_w.reshape(-1)[order][:, None].astype(xt.dtype)
    out = jnp.zeros_like(xt).at[token_of].add(y_assign)
    return out.reshape(shp)


def decoder_layer(x, pos, gla_s0, swa_k_buf, swa_v_buf, w_in, w_gk_up, b_gk_up, gla_norm_g,
                  swa_sinks, w_out, ln1_g, ln1_b, w_router, b_router, w_gate, b_gate,
                  w_up, b_up, w_down, b_down, ln2_g, ln2_b):
    B, L, _ = x.shape
    h = x @ w_in
    q_g, k_g, v_g, o_gate, g_lr, q_s, k_s, v_s = jnp.split(
        h, np.cumsum(IN_SPLITS)[:-1].tolist(), axis=-1)
    gk = jax.nn.log_sigmoid((g_lr @ w_gk_up + b_gk_up).astype(jnp.float32)) / GLA_GATE_NORM
    o_gla, s_new = gla_recurrence(
        q_g.reshape(B, L, GLA_HEADS, GLA_DK) * (GLA_DK ** -0.5),
        k_g.reshape(B, L, GLA_HEADS, GLA_DK),
        v_g.reshape(B, L, GLA_HEADS, GLA_DV),
        gk.reshape(B, L, GLA_HEADS, GLA_DK), gla_s0)
    o_gla = rms_norm_f32(o_gla, gla_norm_g) * jax.nn.silu(
        o_gate.reshape(B, L, GLA_HEADS, GLA_DV).astype(jnp.float32))
    o_gla = o_gla.reshape(B, L, GLA_WIDTH).astype(x.dtype)
    q = rope_partial(q_s.reshape(B, L, SWA_HEADS, HEAD_DIM), pos)
    k = rope_partial(k_s.reshape(B, L, SWA_KV_HEADS, HEAD_DIM), pos)
    v = v_s.reshape(B, L, SWA_KV_HEADS, HEAD_DIM)
    sinks = swa_sinks.reshape(SWA_KV_HEADS, SWA_GROUP)
    if swa_k_buf is None:
        o_swa, k_keep, v_keep = swa_prompt(q, k, v, sinks)
    else:
        o_swa, k_keep, v_keep = swa_sample(q, k, v, swa_k_buf, swa_v_buf, pos, sinks)
    mix = jnp.concatenate([o_gla, o_swa], axis=-1) @ w_out
    x1 = layer_norm(DEEPNORM_ALPHA * x + mix, ln1_g, ln1_b)
    ffn = moe_ffn(x1, w_router, b_router, w_gate, b_gate, w_up, b_up, w_down, b_down)
    x2 = layer_norm(DEEPNORM_ALPHA * x1 + ffn, ln2_g, ln2_b)
    return x2, s_new.astype(x.dtype), k_keep, v_keep


def setup_inputs(seed: int = 0) -> dict:
    key = jax.random.key(seed)
    ks = jax.random.split(key, 24)
    nrm = lambda i, shape, scale: jax.random.normal(ks[i], shape, jnp.float32) * scale
    return {
        'x_prompt': nrm(0, (BATCH, SEQ, D_MODEL), 1.0),
        'x_sample': nrm(1, (DEC_BATCH, DEC_SEQ, D_MODEL), 1.0),
        'state_gla': nrm(2, (DEPTH, DEC_BATCH, GLA_HEADS, GLA_DK, GLA_DV), 0.5),
        'cache_swa_k': nrm(3, (DEPTH, DEC_BATCH, WINDOW, SWA_KV_HEADS, HEAD_DIM), 1.0),
        'cache_swa_v': nrm(4, (DEPTH, DEC_BATCH, WINDOW, SWA_KV_HEADS, HEAD_DIM), 1.0),
        'w_in': nrm(5, (DEPTH, D_MODEL, D_IN), D_MODEL ** -0.5),
        'w_gk_up': nrm(6, (DEPTH, GLA_GATE_RANK, GLA_QK), GLA_GATE_RANK ** -0.5),
        'b_gk_up': nrm(7, (DEPTH, GLA_QK), 0.1),
        'gla_norm_g': 1.0 + nrm(8, (DEPTH, GLA_DV), 0.02),
        'swa_sinks': nrm(9, (DEPTH, SWA_HEADS), 1.0),
        'w_out': nrm(10, (DEPTH, D_MODEL, D_MODEL), DEEPNORM_BETA * D_MODEL ** -0.5),
        'ln1_g': 1.0 + nrm(11, (DEPTH, D_MODEL), 0.02),
        'ln1_b': nrm(12, (DEPTH, D_MODEL), 0.02),
        'w_router': nrm(13, (DEPTH, D_MODEL, N_EXPERTS), D_MODEL ** -0.5),
        'b_router': nrm(14, (DEPTH, N_EXPERTS), 0.01),
        'w_gate': nrm(15, (DEPTH, N_EXPERTS, D_MODEL, EXPERT_DFF), D_MODEL ** -0.5),
        'b_gate': nrm(16, (DEPTH, N_EXPERTS, EXPERT_DFF), 0.02),
        'w_up': nrm(17, (DEPTH, N_EXPERTS, D_MODEL, EXPERT_DFF), D_MODEL ** -0.5),
        'b_up': nrm(18, (DEPTH, N_EXPERTS, EXPERT_DFF), 0.02),
        'w_down': nrm(19, (DEPTH, N_EXPERTS, EXPERT_DFF, D_MODEL), DEEPNORM_BETA * EXPERT_DFF ** -0.5),
        'b_down': nrm(20, (DEPTH, N_EXPERTS, D_MODEL), 0.02),
        'ln2_g': 1.0 + nrm(21, (DEPTH, D_MODEL), 0.02),
        'ln2_b': nrm(22, (DEPTH, D_MODEL), 0.02),
    }


def reference(x_prompt, x_sample, state_gla, cache_swa_k, cache_swa_v, w_in, w_gk_up, b_gk_up,
              gla_norm_g, swa_sinks, w_out, ln1_g, ln1_b, w_router, b_router, w_gate, b_gate,
              w_up, b_up, w_down, b_down, ln2_g, ln2_b):
    pos_prompt = jnp.arange(x_prompt.shape[1])
    pos_sample = PAST_LEN + jnp.arange(x_sample.shape[1])
    s0_prompt = jnp.zeros((x_prompt.shape[0], GLA_HEADS, GLA_DK, GLA_DV), jnp.float32)
    xp, xs = x_prompt, x_sample
    gla_p, gla_s, kp_l, vp_l, ks_l, vs_l = [], [], [], [], [], []
    for l in range(DEPTH):
        weights = (w_in[l], w_gk_up[l], b_gk_up[l], gla_norm_g[l], swa_sinks[l], w_out[l],
                   ln1_g[l], ln1_b[l], w_router[l], b_router[l], w_gate[l], b_gate[l],
                   w_up[l], b_up[l], w_down[l], b_down[l], ln2_g[l], ln2_b[l])
        xp, sp, kp, vp = decoder_layer(xp, pos_prompt, s0_prompt, None, None, *weights)
        xs, ss, ksm, vsm = decoder_layer(xs, pos_sample, state_gla[l], cache_swa_k[l],
                                         cache_swa_v[l], *weights)
        gla_p.append(sp); gla_s.append(ss)
        kp_l.append(kp); vp_l.append(vp); ks_l.append(ksm); vs_l.append(vsm)
    state_gla_prompt = jnp.stack(gla_p)
    state_gla_sample = jnp.stack(gla_s)
    swa_k_prompt = jnp.stack(kp_l)
    swa_v_prompt = jnp.stack(vp_l)
    swa_k_sample = jnp.stack(ks_l)
    swa_v_sample = jnp.stack(vs_l)
    return (xp, xs, state_gla_prompt, state_gla_sample, swa_k_prompt, swa_v_prompt, swa_k_sample, swa_v_sample)
```

```python
import functools

import numpy as np
import jax
import jax.numpy as jnp
from jax import lax
from jax.experimental import pallas as pl
from jax.experimental.pallas import tpu as pltpu

F32 = jnp.float32
BF16 = jnp.bfloat16

D_MODEL = 1024
PAST_LEN = 16384
GLA_HEADS = 4
GLA_DK = 64
GLA_DV = 128
GLA_QK = GLA_HEADS * GLA_DK
GLA_WIDTH = GLA_HEADS * GLA_DV
GLA_GATE_RANK = 16
GLA_GATE_NORM = 16.0
GLA_CHUNK = 64
GLA_SUB = 16
HEAD_DIM = 64
SWA_HEADS = 8
SWA_KV_HEADS = 2
SWA_WIDTH = SWA_HEADS * HEAD_DIM
SWA_KV_WIDTH = SWA_KV_HEADS * HEAD_DIM
WINDOW = 128
ROPE_DIM = 16
ROPE_THETA = 500000.0
N_EXPERTS = 32
TOP_K = 4
SWIGLU_LIMIT = 7.0
SWIGLU_ALPHA = 1.702
NORM_EPS = 1e-5
DEEPNORM_ALPHA = 2.0 ** 0.25

LANES = 128
SAMPLE_PAD = 16
MOE_TM = 256
VMEM_LIMIT = 56 * 1024 * 1024

_C_QK = (0, 512)
_C_VG = (512, 1024)
_C_OG = (1024, 1536)
_C_QS = (1536, 2048)
_C_KV = (2048, 2304)
_C_GLR = (2304, 2432)
D_IN_PAD = 2432


def _cparams(sem):
    return pltpu.CompilerParams(dimension_semantics=sem, vmem_limit_bytes=VMEM_LIMIT)


def _proj_kernel(x_ref, w_ref, wgk_ref, bgk_ref, ca_ref, cb_ref, cc_ref,
                 qg_ref, kg_ref, vg_ref, og_ref, gk_ref, qs_ref, ks_ref, vs_ref):
    x = x_ref[...].astype(BF16)

    def mm(c):
        return jnp.dot(x, w_ref[:, c[0]:c[1]], preferred_element_type=F32)

    qk = mm(_C_QK)
    qg_ref[...] = qk[:, :GLA_QK]
    kg_ref[...] = qk[:, GLA_QK:]
    vg_ref[...] = mm(_C_VG)
    og_ref[...] = mm(_C_OG)
    qs = mm(_C_QS)
    kv = mm(_C_KV)
    glr = mm(_C_GLR)

    z = jnp.dot(glr.astype(BF16), wgk_ref[...], preferred_element_type=F32) + bgk_ref[...]
    gk_ref[...] = -(jnp.maximum(-z, 0.0) + jnp.log1p(jnp.exp(-jnp.abs(z)))) * (1.0 / GLA_GATE_NORM)

    ca, cb, cc = ca_ref[...], cb_ref[...], cc_ref[...]

    def rope(t):
        return t * ca + pltpu.roll(t, LANES - ROPE_DIM // 2, 1) * cb + pltpu.roll(t, ROPE_DIM // 2, 1) * cc

    for j in range(SWA_WIDTH // LANES):
        qs_ref[:, j * LANES:(j + 1) * LANES] = rope(qs[:, j * LANES:(j + 1) * LANES])
    ks_ref[...] = rope(kv[:, :SWA_KV_WIDTH])
    vs_ref[...] = kv[:, SWA_KV_WIDTH:]


def _proj(x, w_r, wgk, bgk, tabs, tm):
    t = x.shape[0]
    nb_tab = tabs[0].shape[0] // tm
    row = lambda i: (i, 0)
    const = lambda i: (0, 0)
    tab = lambda i: (i % nb_tab, 0)
    widths = (GLA_QK, GLA_QK, GLA_WIDTH, GLA_WIDTH, GLA_QK, SWA_WIDTH, SWA_KV_WIDTH, SWA_KV_WIDTH)
    return pl.pallas_call(
        _proj_kernel,
        grid=(t // tm,),
        in_specs=[pl.BlockSpec((tm, D_MODEL), row),
                  pl.BlockSpec((D_MODEL, D_IN_PAD), const),
                  pl.BlockSpec((LANES, GLA_QK), const),
                  pl.BlockSpec((1, GLA_QK), const),
                  pl.BlockSpec((tm, LANES), tab),
                  pl.BlockSpec((tm, LANES), tab),
                  pl.BlockSpec((tm, LANES), tab)],
        out_specs=[pl.BlockSpec((tm, w), row) for w in widths],
        out_shape=[jax.ShapeDtypeStruct((t, w), F32) for w in widths],
        compiler_params=_cparams(("parallel",)),
        name="proj",
    )(x, w_r, wgk, bgk, *tabs)


def _rope_tables(pos):
    half = ROPE_DIM // 2
    freqs = ROPE_THETA ** (-jnp.arange(half, dtype=F32) * 2.0 / ROPE_DIM)
    ang = pos.astype(F32)[:, None] * freqs[None, :]
    cos, sin = jnp.cos(ang), jnp.sin(ang)
    n = pos.shape[0]
    rest = HEAD_DIM - ROPE_DIM
    a = jnp.concatenate([cos, cos, jnp.ones((n, rest), F32)], axis=1)
    b = jnp.concatenate([-sin, jnp.zeros((n, half + rest), F32)], axis=1)
    c = jnp.concatenate([jnp.zeros((n, half), F32), sin, jnp.zeros((n, rest), F32)], axis=1)
    rep = LANES // HEAD_DIM
    return tuple(jnp.tile(t, (1, rep)) for t in (a, b, c))


def _split3(a):
    a1 = a.astype(BF16)
    r1 = a - a1.astype(F32)
    a2 = r1.astype(BF16)
    a3 = (r1 - a2.astype(F32)).astype(BF16)
    return a1, a2, a3


_NT = (((1,), (1,)), ((), ()))
_TN = (((0,), (0,)), ((), ()))


def _gla_chunk(q, k, g, v, og, s_pairs, ng, ind, c, valid):
    assert c in (GLA_SUB, GLA_DK)
    nsub = c // GLA_SUB
    row = lax.broadcasted_iota(jnp.int32, (c, 1), 0)
    if valid < c:
        live = row < valid
        g = jnp.where(live, g, 0.0)
        k = jnp.where(live, k, 0.0)
        v = jnp.where(live, v, 0.0)
    q = q * (GLA_DK ** -0.5)

    tri = jnp.where(lax.broadcasted_iota(jnp.int32, (c, c), 0) >= lax.broadcasted_iota(jnp.int32, (c, c), 1),
                    1.0, 0.0).astype(BF16)
    g3 = _split3(g)
    b = sum(jnp.dot(tri, gi, preferred_element_type=F32) for gi in g3)
    ones = jnp.ones((c, LANES), BF16)

    lane = lax.broadcasted_iota(jnp.int32, (1, LANES), 1)
    half = (lane < GLA_DK, lane >= GLA_DK)
    col = lane & (GLA_DK - 1)
    sub_of = lambda t: jnp.right_shift(t, 4)
    diag_keep = (sub_of(col) == sub_of(row)) & (col >= row) & (col < c)

    outs = []
    new_s = []
    for p in range(GLA_HEADS // 2):
        sl = slice(p * LANES, (p + 1) * LANES)
        qp, kp, bp = q[:, sl], k[:, sl], b[:, sl]
        s_p = s_pairs[p]

        def sub_rows(t, off):
            return jnp.concatenate(
                [jnp.broadcast_to(t[GLA_SUB * j + off:GLA_SUB * j + off + 1, :], (GLA_SUB, LANES))
                 for j in range(nsub)], axis=0)

        tiles = []
        for tl in range(GLA_SUB):
            w = kp * sub_rows(qp, tl) * jnp.exp(jnp.minimum(sub_rows(bp, tl) - bp, 0.0))
            tiles.append(w.astype(BF16))
        att_t = jnp.dot(jnp.concatenate(tiles, axis=1), ind, preferred_element_type=F32)
        att_t = jnp.where(diag_keep, att_t, 0.0)

        if nsub > 1:
            qt = qp * jnp.exp(jnp.minimum(bp - sub_rows(bp, 0), 0.0))
            kts = []
            for i in range(1, nsub):
                r_i = bp[GLA_SUB * i:GLA_SUB * i + 1, :]
                kts.append(jnp.where(row < GLA_SUB * i,
                                     kp * jnp.exp(jnp.minimum(r_i - bp, 0.0)), 0.0).astype(BF16))
            lhs = jnp.concatenate(kts, axis=1)
            rhs_rows = []
            for hh in range(2):
                qh = jnp.where(half[hh], qt, 0.0)
                rhs_rows.append(jnp.concatenate(
                    [jnp.where(sub_of(row) == i, qh, 0.0).astype(BF16) for i in range(1, nsub)], axis=1))
            rhs = jnp.concatenate(rhs_rows, axis=0)
            att_t = att_t + lax.dot_general(lhs, rhs, _NT, preferred_element_type=F32)

        att_b = att_t.astype(BF16)
        qe = qp * jnp.exp(bp)
        kd = kp * jnp.exp(bp[c - 1:c, :] - bp)
        upd = jnp.zeros((LANES, GLA_DV), F32)
        for hh in range(2):
            h = 2 * p + hh
            vh = v[:, h * GLA_DV:(h + 1) * GLA_DV].astype(BF16)
            z = lax.dot_general(att_b, vh, _TN, preferred_element_type=F32)
            o_intra = z[hh * GLA_DK:hh * GLA_DK + c, :]
            o_inter = jnp.dot(jnp.where(half[hh], qe, 0.0).astype(BF16), s_p.astype(BF16),
                              preferred_element_type=F32)
            o = o_inter + o_intra
            o = o * lax.rsqrt(jnp.mean(o * o, axis=-1, keepdims=True) + NORM_EPS) * ng
            gate = og[:, h * GLA_DV:(h + 1) * GLA_DV]
            outs.append(o * (gate * jax.nn.sigmoid(gate)))
            upd = upd + lax.dot_general(jnp.where(half[hh], kd, 0.0).astype(BF16), vh, _TN,
                                        preferred_element_type=F32)
        dcol = sum(lax.dot_general(gi[:, sl], ones, _TN, preferred_element_type=F32) for gi in g3)
        new_s.append(s_p * jnp.exp(dcol) + upd)
    return jnp.concatenate(outs, axis=1), new_s


def _gla_kernel(q_ref, k_ref, g_ref, v_ref, og_ref, s0_ref, ng_ref, ind_ref,
                o_ref, sn_ref, s_scr, *, sb, nc, c, valid, out_rows):
    tb = pl.program_id(1)

    @pl.when(tb == 0)
    def _():
        s_scr[...] = s0_ref[...]

    ng = ng_ref[...]
    ind = ind_ref[...]

    def seq_body(sq, carry):
        def chunk_body(ci, carry2):
            r0 = pl.multiple_of(ci * c, c)
            rows = pl.ds(r0, c)
            s_pairs = [s_scr[sq, p] for p in range(GLA_HEADS // 2)]
            o, new_s = _gla_chunk(q_ref[sq, rows, :], k_ref[sq, rows, :], g_ref[sq, rows, :],
                                  v_ref[sq, rows, :], og_ref[sq, rows, :], s_pairs, ng, ind, c, valid)
            for p in range(GLA_HEADS // 2):
                s_scr[sq, p] = new_s[p]
            if out_rows == c:
                o_ref[sq, rows, :] = o
            else:
                o_ref[sq, :, :] = o[:out_rows, :]
            return carry2

        return lax.fori_loop(0, nc, chunk_body, carry)

    lax.fori_loop(0, sb, seq_body, 0)

    @pl.when(tb == pl.num_programs(1) - 1)
    def _():
        sn_ref[...] = s_scr[...]


def _gla_indicator():
    k = np.arange(GLA_SUB * LANES)
    n = np.arange(LANES)
    tl, hf = k // LANES, (k % LANES) // GLA_DK
    m = (hf[:, None] == (n // GLA_DK)[None, :]) & ((n % GLA_SUB)[None, :] == tl[:, None])
    return jnp.asarray(m, dtype=BF16)


def _gla(q, k, g, v, og, s0, ng, *, c, valid, sb, nc, out_rows):
    ns, l, _ = q.shape
    tb = nc * c
    ob = out_rows if out_rows != c else tb
    ol = l if out_rows == c else out_rows
    seq = lambda i, j: (i, j, 0)
    st = lambda i, j: (i, 0, 0, 0)
    kern = functools.partial(_gla_kernel, sb=sb, nc=nc, c=c, valid=valid, out_rows=out_rows)
    return pl.pallas_call(
        kern,
        grid=(ns // sb, l // tb),
        in_specs=[pl.BlockSpec((sb, tb, GLA_QK), seq)] * 3
                 + [pl.BlockSpec((sb, tb, GLA_WIDTH), seq)] * 2
                 + [pl.BlockSpec((sb, 2, LANES, GLA_DV), st),
                    pl.BlockSpec((1, GLA_DV), lambda i, j: (0, 0)),
                    pl.BlockSpec((GLA_SUB * LANES, LANES), lambda i, j: (0, 0))],
        out_specs=[pl.BlockSpec((sb, ob, GLA_WIDTH), seq),
                   pl.BlockSpec((sb, 2, LANES, GLA_DV), st)],
        out_shape=[jax.ShapeDtypeStruct((ns, ol, GLA_WIDTH), F32),
                   jax.ShapeDtypeStruct((ns, 2, LANES, GLA_DV), F32)],
        scratch_shapes=[pltpu.VMEM((sb, 2, LANES, GLA_DV), F32)],
        compiler_params=_cparams(("parallel", "arbitrary")),
        name="gla",
    )(q, k, g, v, og, s0, ng, _gla_indicator())


def _place(t, lane_lo, src_half, dst_half):
    keep = lane_lo if src_half == 0 else jnp.logical_not(lane_lo)
    t = jnp.where(keep, t, 0.0)
    if src_half != dst_half:
        t = pltpu.roll(t, HEAD_DIM, 1)
    return t


def _swa_prompt_kernel(sink_ref, q_ref, kc_ref, kp_ref, vc_ref, vp_ref, o_ref):
    n = pl.program_id(1)
    kband = jnp.concatenate([kp_ref[0], kc_ref[0]], axis=0)
    vband = jnp.concatenate([vp_ref[0], vc_ref[0]], axis=0)
    lane_lo = lax.broadcasted_iota(jnp.int32, (1, LANES), 1) < HEAD_DIM
    qpos = n * WINDOW + lax.broadcasted_iota(jnp.int32, (WINDOW, 1), 0)
    kpos = (n - 1) * WINDOW + lax.broadcasted_iota(jnp.int32, (1, 2 * WINDOW), 1)
    dist = qpos - kpos
    mask = (dist >= 0) & (dist < WINDOW) & (kpos >= 0)
    for gi in range(SWA_KV_HEADS):
        kalt = [_place(kband, lane_lo, gi, hh).astype(BF16) for hh in range(2)]
        valt = [_place(vband, lane_lo, gi, hh).astype(BF16) for hh in range(2)]
        for pp in range(2):
            pidx = 2 * gi + pp
            qpair = q_ref[0, :, pidx * LANES:(pidx + 1) * LANES].astype(BF16)
            acc = jnp.zeros((WINDOW, LANES), F32)
            for hh in range(2):
                sink = sink_ref[2 * pidx + hh]
                s = lax.dot_general(qpair, kalt[hh], _NT, preferred_element_type=F32) * (HEAD_DIM ** -0.5)
                s = jnp.where(mask, s, -jnp.inf)
                m = jnp.maximum(jnp.max(s, axis=-1, keepdims=True), sink)
                p = jnp.exp(s - m)
                den = jnp.sum(p, axis=-1, keepdims=True) + jnp.exp(sink - m)
                acc = acc + jnp.dot(p.astype(BF16), valt[hh], preferred_element_type=F32) / den
            o_ref[0, :, pidx * LANES:(pidx + 1) * LANES] = acc


def _swa_prompt(q, k, v, sinks):
    b, l, _ = q.shape
    cur = lambda i, j, s: (i, j, 0)
    prev = lambda i, j, s: (i, jnp.maximum(j - 1, 0), 0)
    return pl.pallas_call(
        _swa_prompt_kernel,
        grid_spec=pltpu.PrefetchScalarGridSpec(
            num_scalar_prefetch=1,
            grid=(b, l // WINDOW),
            in_specs=[pl.BlockSpec((1, WINDOW, SWA_WIDTH), cur),
                      pl.BlockSpec((1, WINDOW, SWA_KV_WIDTH), cur),
                      pl.BlockSpec((1, WINDOW, SWA_KV_WIDTH), prev),
                      pl.BlockSpec((1, WINDOW, SWA_KV_WIDTH), cur),
                      pl.BlockSpec((1, WINDOW, SWA_KV_WIDTH), prev)],
            out_specs=pl.BlockSpec((1, WINDOW, SWA_WIDTH), cur)),
        out_shape=jax.ShapeDtypeStruct((b, l, SWA_WIDTH), F32),
        compiler_params=_cparams(("parallel", "parallel")),
        name="swa_prompt",
    )(sinks, q, k, k, v, v)


_QROWS = 8


def _swa_sample_kernel(sink_ref, q_ref, kn_ref, vn_ref, ck_ref, cv_ref, o_ref, *, bb, ntok):
    lane_lo = lax.broadcasted_iota(jnp.int32, (1, LANES), 1) < HEAD_DIM
    nrow = 4 * _QROWS
    rowi = lax.broadcasted_iota(jnp.int32, (nrow, 1), 0)
    t = rowi & (_QROWS - 1)
    blk_of_row = jnp.right_shift(rowi, 3)
    j = lax.broadcasted_iota(jnp.int32, (1, 2 * WINDOW), 1)
    jn = j - WINDOW
    mask = ((j < WINDOW) & (j > t)) | ((jn >= 0) & (jn <= t) & (jn < ntok))
    zpad = jnp.zeros((WINDOW - _QROWS, LANES), F32)

    def body(bi, carry):
        kk = jnp.concatenate([ck_ref[bi], kn_ref[bi, 0:_QROWS, :], zpad], axis=0)
        vv = jnp.concatenate([cv_ref[bi], vn_ref[bi, 0:_QROWS, :], zpad], axis=0)
        for gi in range(SWA_KV_HEADS):
            kboth = (_place(kk, lane_lo, gi, 0) + _place(kk, lane_lo, gi, 1)).astype(BF16)
            vboth = (_place(vv, lane_lo, gi, 0) + _place(vv, lane_lo, gi, 1)).astype(BF16)
            pieces = []
            sink_col = jnp.zeros((nrow, 1), F32)
            for pp in range(2):
                pidx = 2 * gi + pp
                qpair = q_ref[bi, 0:_QROWS, pidx * LANES:(pidx + 1) * LANES]
                for hh in range(2):
                    keep = lane_lo if hh == 0 else jnp.logical_not(lane_lo)
                    pieces.append(jnp.where(keep, qpair, 0.0))
                    blk = 2 * pp + hh
                    sink_col = jnp.where(blk_of_row == blk, sink_ref[2 * pidx + hh], sink_col)
            qst = jnp.concatenate(pieces, axis=0).astype(BF16)
            s = lax.dot_general(qst, kboth, _NT, preferred_element_type=F32) * (HEAD_DIM ** -0.5)
            s = jnp.where(mask, s, -jnp.inf)
            m = jnp.maximum(jnp.max(s, axis=-1, keepdims=True), sink_col)
            p = jnp.exp(s - m)
            den = jnp.sum(p, axis=-1, keepdims=True) + jnp.exp(sink_col - m)
            pv = jnp.dot(p.astype(BF16), vboth, preferred_element_type=F32) / den
            for pp in range(2):
                pidx = 2 * gi + pp
                r0 = 2 * pp * _QROWS
                o = jnp.where(lane_lo, pv[r0:r0 + _QROWS, :], pv[r0 + _QROWS:r0 + 2 * _QROWS, :])
                o_ref[bi, :, pidx * LANES:(pidx + 1) * LANES] = o[:ntok, :]
        return carry

    lax.fori_loop(0, bb, body, 0)


def _swa_sample(q, kn, vn, ck, cv, sinks, ntok, bb):
    b = q.shape[0]
    blk = lambda i, s: (i, 0, 0)
    return pl.pallas_call(
        functools.partial(_swa_sample_kernel, bb=bb, ntok=ntok),
        grid_spec=pltpu.PrefetchScalarGridSpec(
            num_scalar_prefetch=1,
            grid=(b // bb,),
            in_specs=[pl.BlockSpec((bb, SAMPLE_PAD, SWA_WIDTH), blk),
                      pl.BlockSpec((bb, SAMPLE_PAD, SWA_KV_WIDTH), blk),
                      pl.BlockSpec((bb, SAMPLE_PAD, SWA_KV_WIDTH), blk),
                      pl.BlockSpec((bb, WINDOW, SWA_KV_WIDTH), blk),
                      pl.BlockSpec((bb, WINDOW, SWA_KV_WIDTH), blk)],
            out_specs=pl.BlockSpec((bb, ntok, SWA_WIDTH), blk)),
        out_shape=jax.ShapeDtypeStruct((b, ntok, SWA_WIDTH), F32),
        compiler_params=_cparams(("parallel",)),
        name="swa_sample",
    )(sinks, q, kn, vn, ck, cv)


def _layer_norm(y, g, b):
    mu = jnp.mean(y, axis=-1, keepdims=True)
    yc = y - mu
    var = jnp.mean(yc * yc, axis=-1, keepdims=True)
    return yc * lax.rsqrt(var + NORM_EPS) * g + b


def _mix_kernel(ogp_ref, osp_ref, xp_ref, ogs_ref, oss_ref, xs_ref, wo_ref, g_ref, b_ref,
                wrh_ref, wrl_ref, br_ref, x1_ref, ri_ref, rw_ref, cnt_ref, carry, *, nb_prompt, tm):
    i = pl.program_id(0)

    @pl.when(i == 0)
    def _():
        carry[...] = jnp.zeros_like(carry)

    is_s = i >= nb_prompt
    og = jnp.where(is_s, ogs_ref[...], ogp_ref[...]).astype(BF16)
    os_ = jnp.where(is_s, oss_ref[...], osp_ref[...]).astype(BF16)
    x = jnp.where(is_s, xs_ref[...], xp_ref[...])
    mix = (jnp.dot(og, wo_ref[0:GLA_WIDTH, :], preferred_element_type=F32)
           + jnp.dot(os_, wo_ref[GLA_WIDTH:, :], preferred_element_type=F32))
    x1 = _layer_norm(DEEPNORM_ALPHA * x + mix, g_ref[...], b_ref[...])
    x1_ref[...] = x1

    xh = x1.astype(BF16)
    xl = (x1 - xh.astype(F32)).astype(BF16)
    logits = (jnp.dot(xh, wrh_ref[...], preferred_element_type=F32)
              + jnp.dot(xh, wrl_ref[...], preferred_element_type=F32)
              + jnp.dot(xl, wrh_ref[...], preferred_element_type=F32)) + br_ref[...]

    lane = lax.broadcasted_iota(jnp.int32, (1, LANES), 1)
    lane_f = lane.astype(F32)
    vals = logits
    tops, idxs, hots = [], [], []
    for _ in range(TOP_K):
        m = jnp.max(vals, axis=-1, keepdims=True)
        idx = jnp.min(jnp.where(vals == m, lane_f, float(LANES)), axis=-1, keepdims=True)
        hot = lane_f == idx
        tops.append(m)
        idxs.append(idx)
        hots.append(hot)
        vals = jnp.where(hot, -jnp.inf, vals)
    es = [jnp.exp(tv - tops[0]) for tv in tops]
    den = es[0] + es[1] + es[2] + es[3]

    multi = sum(jnp.where(h, 1.0, 0.0) for h in hots)
    tri = jnp.where(lax.broadcasted_iota(jnp.int32, (tm, tm), 0) > lax.broadcasted_iota(jnp.int32, (tm, tm), 1),
                    1.0, 0.0).astype(BF16)
    cum = jnp.dot(tri, multi.astype(BF16), preferred_element_type=F32)
    base = cum + carry[...]
    ri = jnp.zeros((tm, LANES), jnp.int32)
    rw = jnp.zeros((tm, LANES), F32)
    for k in range(TOP_K):
        rank = jnp.sum(jnp.where(hots[k], base, 0.0), axis=-1, keepdims=True).astype(jnp.int32)
        ri = jnp.where(lane == k, idxs[k].astype(jnp.int32), ri)
        ri = jnp.where(lane == TOP_K + k, rank, ri)
        rw = jnp.where(lane == k, es[k] / den, rw)
    ri_ref[...] = ri
    rw_ref[...] = rw
    carry[...] = carry[...] + jnp.sum(multi, axis=0, keepdims=True)
    cnt_ref[...] = carry[...]


def _mix(og_p, os_p, x_p, og_s, os_s, x_s, wo, g, b, wrh, wrl, br, tm):
    tp, ts = x_p.shape[0], x_s.shape[0]
    nbp = tp // tm
    t_all = tp + ts
    prow = lambda i: (jnp.minimum(i, nbp - 1), 0)
    srow = lambda i: (jnp.maximum(i - nbp, 0), 0)
    row = lambda i: (i, 0)
    const = lambda i: (0, 0)
    return pl.pallas_call(
        functools.partial(_mix_kernel, nb_prompt=nbp, tm=tm),
        grid=(t_all // tm,),
        in_specs=[pl.BlockSpec((tm, GLA_WIDTH), prow), pl.BlockSpec((tm, SWA_WIDTH), prow),
                  pl.BlockSpec((tm, D_MODEL), prow),
                  pl.BlockSpec((tm, GLA_WIDTH), srow), pl.BlockSpec((tm, SWA_WIDTH), srow),
                  pl.BlockSpec((tm, D_MODEL), srow),
                  pl.BlockSpec((D_MODEL, D_MODEL), const),
                  pl.BlockSpec((1, D_MODEL), const), pl.BlockSpec((1, D_MODEL), const),
                  pl.BlockSpec((D_MODEL, LANES), const), pl.BlockSpec((D_MODEL, LANES), const),
                  pl.BlockSpec((1, LANES), const)],
        out_specs=[pl.BlockSpec((tm, D_MODEL), row), pl.BlockSpec((tm, LANES), row),
                   pl.BlockSpec((tm, LANES), row), pl.BlockSpec((1, LANES), const)],
        out_shape=[jax.ShapeDtypeStruct((t_all, D_MODEL), F32),
                   jax.ShapeDtypeStruct((t_all, LANES), jnp.int32),
                   jax.ShapeDtypeStruct((t_all, LANES), F32),
                   jax.ShapeDtypeStruct((1, LANES), F32)],
        scratch_shapes=[pltpu.VMEM((1, LANES), F32)],
        compiler_params=_cparams(("arbitrary",)),
        name="mix",
    )(og_p, os_p, x_p, og_s, os_s, x_s, wo, g, b, wrh, wrl, br)


def _moe_kernel(be_ref, na_ref, tok_cur, tok_nxt, pos_cur, x_hbm, wg_ref, wu_ref, wd_ref,
                bg_ref, bu_ref, bd_ref, y_hbm, xbuf, ybuf, wbf, gsem, ssem, *, tm):
    j = pl.program_id(0)
    n_act = na_ref[0]
    slot = j % 2

    def gather(tok_ref, dst_slot):
        def start(r, c):
            pltpu.make_async_copy(x_hbm.at[pl.ds(tok_ref[0, 0, r], 1), :],
                                  xbuf.at[dst_slot, pl.ds(r, 1), :], gsem.at[dst_slot]).start()
            return c
        lax.fori_loop(0, tm, start, 0, unroll=8)

    def wait_rows(src, dst, sem):
        def w(r, c):
            pltpu.make_async_copy(src.at[pl.ds(0, 1), :], dst.at[pl.ds(0, 1), :], sem).wait()
            return c
        lax.fori_loop(0, tm, w, 0, unroll=8)

    @pl.when(j == 0)
    def _():
        gather(tok_cur, 0)
        ybuf[...] = jnp.zeros_like(ybuf)
        dump = pltpu.make_async_copy(ybuf, y_hbm.at[pl.ds(y_hbm.shape[0] - tm, tm), :], ssem)
        dump.start()
        dump.wait()

    @pl.when(j < n_act)
    def _():
        @pl.when(j + 1 < n_act)
        def _():
            gather(tok_nxt, 1 - slot)

        changed = jnp.logical_or(j == 0, be_ref[j] != be_ref[jnp.maximum(j - 1, 0)])

        @pl.when(changed)
        def _():
            wbf[0] = wg_ref[0].astype(BF16)
            wbf[1] = wu_ref[0].astype(BF16)
            wbf[2] = wd_ref[0].astype(BF16)

        wait_rows(x_hbm, xbuf.at[slot], gsem.at[slot])
        x = xbuf[slot].astype(BF16)
        gate = jnp.dot(x, wbf[0], preferred_element_type=F32) + bg_ref[0]
        up = jnp.dot(x, wbf[1], preferred_element_type=F32) + bu_ref[0]
        gate = jnp.minimum(gate, SWIGLU_LIMIT)
        up = jnp.clip(up, -SWIGLU_LIMIT, SWIGLU_LIMIT)
        h = gate * jax.nn.sigmoid(SWIGLU_ALPHA * gate) * (up + 1.0)
        y = jnp.dot(h.astype(BF16), wbf[2], preferred_element_type=F32) + bd_ref[0]

        @pl.when(j > 0)
        def _():
            wait_rows(ybuf, y_hbm, ssem)

        ybuf[...] = y

        def scat(r, c):
            pltpu.make_async_copy(ybuf.at[pl.ds(r, 1), :],
                                  y_hbm.at[pl.ds(pos_cur[0, 0, r], 1), :], ssem).start()
            return c
        lax.fori_loop(0, tm, scat, 0, unroll=8)

        @pl.when(j == n_act - 1)
        def _():
            wait_rows(ybuf, y_hbm, ssem)


def _moe(x1, tok_of, inv_pos, blk_expert, n_active, wg, wu, wd, bg, bu, bd, tm):
    nslot = tok_of.shape[0]
    nbm = nslot // tm
    n_out = x1.shape[0] * TOP_K + tm
    tok3 = tok_of.reshape(nbm, 1, tm)
    pos3 = inv_pos.reshape(nbm, 1, tm)
    cur = lambda j, be, na: (j, 0, 0)
    nxt = lambda j, be, na: (jnp.minimum(j + 1, nbm - 1), 0, 0)
    wmap = lambda j, be, na: (be[j], 0, 0)
    smem = functools.partial(pl.BlockSpec, memory_space=pltpu.SMEM)
    return pl.pallas_call(
        functools.partial(_moe_kernel, tm=tm),
        grid_spec=pltpu.PrefetchScalarGridSpec(
            num_scalar_prefetch=2,
            grid=(nbm,),
            in_specs=[smem((1, 1, tm), cur), smem((1, 1, tm), nxt), smem((1, 1, tm), cur),
                      pl.BlockSpec(memory_space=pl.ANY),
                      pl.BlockSpec((1, D_MODEL, D_MODEL), wmap),
                      pl.BlockSpec((1, D_MODEL, D_MODEL), wmap),
                      pl.BlockSpec((1, D_MODEL, D_MODEL), wmap),
                      pl.BlockSpec((1, 1, D_MODEL), wmap),
                      pl.BlockSpec((1, 1, D_MODEL), wmap),
                      pl.BlockSpec((1, 1, D_MODEL), wmap)],
            out_specs=pl.BlockSpec(memory_space=pl.ANY),
            scratch_shapes=[pltpu.VMEM((2, tm, D_MODEL), F32),
                            pltpu.VMEM((tm, D_MODEL), F32),
                            pltpu.VMEM((3, D_MODEL, D_MODEL), BF16),
                            pltpu.SemaphoreType.DMA((2,)),
                            pltpu.SemaphoreType.DMA(())]),
        out_shape=jax.ShapeDtypeStruct((n_out, D_MODEL), F32),
        compiler_params=_cparams(("arbitrary",)),
        name="moe",
    )(blk_expert, n_active, tok3, tok3, pos3, x1, wg, wu, wd,
      bg.reshape(N_EXPERTS, 1, D_MODEL), bu.reshape(N_EXPERTS, 1, D_MODEL),
      bd.reshape(N_EXPERTS, 1, D_MODEL))


def _combine_kernel(y_ref, rw_ref, x1_ref, g_ref, b_ref, o_ref):
    rw = rw_ref[...]
    ffn = rw[:, 0:1] * y_ref[:, 0:D_MODEL]
    for k in range(1, TOP_K):
        ffn = ffn + rw[:, k:k + 1] * y_ref[:, k * D_MODEL:(k + 1) * D_MODEL]
    o_ref[...] = _layer_norm(DEEPNORM_ALPHA * x1_ref[...] + ffn, g_ref[...], b_ref[...])


def _combine(y4, rw, x1, g, b, row0, nrows, tm):
    off = row0 // tm
    row = lambda i: (i + off, 0)
    const = lambda i: (0, 0)
    return pl.pallas_call(
        _combine_kernel,
        grid=(nrows // tm,),
        in_specs=[pl.BlockSpec((tm, TOP_K * D_MODEL), row), pl.BlockSpec((tm, LANES), row),
                  pl.BlockSpec((tm, D_MODEL), row),
                  pl.BlockSpec((1, D_MODEL), const), pl.BlockSpec((1, D_MODEL), const)],
        out_specs=pl.BlockSpec((tm, D_MODEL), lambda i: (i, 0)),
        out_shape=jax.ShapeDtypeStruct((nrows, D_MODEL), F32),
        compiler_params=_cparams(("parallel",)),
        name="combine",
    )(y4, rw, x1, g, b)


def _route_tables(ri, counts_f, t_all, tm):
    a = t_all * TOP_K
    nbm = a // tm + N_EXPERTS
    nslot = nbm * tm
    counts = counts_f[0, :N_EXPERTS].astype(jnp.int32)
    padded = (counts + tm - 1) // tm * tm
    pad_end = jnp.cumsum(padded)
    pad_start = pad_end - padded
    idx = ri[:, 0:TOP_K]
    rank = ri[:, TOP_K:2 * TOP_K]
    dest = (pad_start[idx] + rank).reshape(-1)
    flat = jnp.arange(a, dtype=jnp.int32)
    inv = jnp.full((nslot,), -1, jnp.int32).at[dest].set(flat)
    is_pad = inv < 0
    inv_pos = jnp.where(is_pad, a + jnp.arange(nslot, dtype=jnp.int32) % tm, inv)
    tok_of = jnp.where(is_pad, 0, inv // TOP_K)
    blk_row0 = jnp.arange(nbm, dtype=jnp.int32) * tm
    blk_expert = jnp.minimum(
        jnp.sum((pad_end[None, :] <= blk_row0[:, None]).astype(jnp.int32), axis=1), N_EXPERTS - 1)
    n_active = (pad_end[-1:] // tm).astype(jnp.int32)
    return tok_of, inv_pos, blk_expert, n_active


def kernel(x_prompt, x_sample, state_gla, cache_swa_k, cache_swa_v, w_in, w_gk_up, b_gk_up, gla_norm_g, swa_sinks, w_out, ln1_g, ln1_b, w_router, b_router, w_gate, b_gate, w_up, b_up, w_down, b_down, ln2_g, ln2_b):
    bp, lp, _ = x_prompt.shape
    bs, ls, _ = x_sample.shape
    tp, ts = bp * lp, bs * ls
    t_all = tp + ts
    tm = 512

    w = w_in[0]
    n_main = GLA_QK * 2 + GLA_WIDTH * 2
    w_r = jnp.concatenate([w[:, :n_main], w[:, n_main + GLA_GATE_RANK:], w[:, n_main:n_main + GLA_GATE_RANK],
                           jnp.zeros((D_MODEL, LANES - GLA_GATE_RANK), F32)], axis=1).astype(BF16)
    wgk = jnp.concatenate([w_gk_up[0], jnp.zeros((LANES - GLA_GATE_RANK, GLA_QK), F32)], axis=0).astype(BF16)
    bgk = b_gk_up[0].reshape(1, GLA_QK)
    ng = gla_norm_g[0].reshape(1, GLA_DV)
    sinks = swa_sinks[0]
    wo = w_out[0].astype(BF16)
    wr = jnp.concatenate([w_router[0], jnp.zeros((D_MODEL, LANES - N_EXPERTS), F32)], axis=1)
    wrh = wr.astype(BF16)
    wrl = (wr - wrh.astype(F32)).astype(BF16)
    br = jnp.concatenate([b_router[0], jnp.full((LANES - N_EXPERTS,), -1e30, F32)]).reshape(1, LANES)

    xp = x_prompt.reshape(tp, D_MODEL)
    tabs_p = _rope_tables(jnp.arange(lp))
    qg, kg, vg, og, gk, qs, ks, vs = _proj(xp, w_r, wgk, bgk, tabs_p, tm)
    r3 = lambda t: t.reshape(bp, lp, t.shape[-1])
    s0p = jnp.zeros((bp, 2, LANES, GLA_DV), F32)
    ogla_p, sn_p = _gla(r3(qg), r3(kg), r3(gk), r3(vg), r3(og), s0p, ng,
                        c=GLA_CHUNK, valid=GLA_CHUNK, sb=1, nc=8, out_rows=GLA_CHUNK)
    oswa_p = _swa_prompt(r3(qs), r3(ks), r3(vs), sinks)
    k_keep_p = ks.reshape(bp, lp, SWA_KV_HEADS, HEAD_DIM)[:, lp - WINDOW:]
    v_keep_p = vs.reshape(bp, lp, SWA_KV_HEADS, HEAD_DIM)[:, lp - WINDOW:]

    xs_pad = jnp.pad(x_sample, ((0, 0), (0, SAMPLE_PAD - ls), (0, 0))).reshape(bs * SAMPLE_PAD, D_MODEL)
    pos_s = PAST_LEN + jnp.arange(SAMPLE_PAD)
    tabs_s = tuple(jnp.tile(t, (tm // SAMPLE_PAD, 1)) for t in _rope_tables(pos_s))
    qg2, kg2, vg2, og2, gk2, qs2, ks2, vs2 = _proj(xs_pad, w_r, wgk, bgk, tabs_s, tm)
    r3s = lambda t: t.reshape(bs, SAMPLE_PAD, t.shape[-1])
    s0s = state_gla[0].reshape(bs, 2, LANES, GLA_DV)
    ogla_s, sn_s = _gla(r3s(qg2), r3s(kg2), r3s(gk2), r3s(vg2), r3s(og2), s0s, ng,
                        c=SAMPLE_PAD, valid=ls, sb=8, nc=1, out_rows=ls)
    ck = cache_swa_k[0].reshape(bs, WINDOW, SWA_KV_WIDTH)
    cv = cache_swa_v[0].reshape(bs, WINDOW, SWA_KV_WIDTH)
    oswa_s = _swa_sample(r3s(qs2), r3s(ks2), r3s(vs2), ck, cv, sinks, ls, 8)
    k_new = r3s(ks2)[:, :ls].reshape(bs, ls, SWA_KV_HEADS, HEAD_DIM)
    v_new = r3s(vs2)[:, :ls].reshape(bs, ls, SWA_KV_HEADS, HEAD_DIM)
    k_keep_s = jnp.concatenate([cache_swa_k[0][:, ls:], k_new], axis=1)
    v_keep_s = jnp.concatenate([cache_swa_v[0][:, ls:], v_new], axis=1)

    x1, ri, rw, cnt = _mix(ogla_p.reshape(tp, GLA_WIDTH), oswa_p.reshape(tp, SWA_WIDTH), xp,
                           ogla_s.reshape(ts, GLA_WIDTH), oswa_s.reshape(ts, SWA_WIDTH),
                           x_sample.reshape(ts, D_MODEL), wo,
                           ln1_g[0].reshape(1, D_MODEL), ln1_b[0].reshape(1, D_MODEL), wrh, wrl, br, tm)
    tok_of, inv_pos, blk_expert, n_active = _route_tables(ri, cnt, t_all, MOE_TM)
    y = _moe(x1, tok_of, inv_pos, blk_expert, n_active, w_gate[0], w_up[0], w_down[0],
             b_gate[0], b_up[0], b_down[0], MOE_TM)
    y4 = y.reshape(y.shape[0] // TOP_K, TOP_K * D_MODEL)
    g2, b2 = ln2_g[0].reshape(1, D_MODEL), ln2_b[0].reshape(1, D_MODEL)
    y_p = _combine(y4, rw, x1, g2, b2, 0, tp, tm)
    y_s = _combine(y4, rw, x1, g2, b2, tp, ts, tm)

    return (y_p.reshape(bp, lp, D_MODEL),
            y_s.reshape(bs, ls, D_MODEL),
            sn_p.reshape(1, bp, GLA_HEADS, GLA_DK, GLA_DV),
            sn_s.reshape(1, bs, GLA_HEADS, GLA_DK, GLA_DV),
            k_keep_p[None], v_keep_p[None], k_keep_s[None], v_keep_s[None])
```

```python
import functools

import numpy as np
import jax
import jax.numpy as jnp
from jax import lax
from jax.experimental import pallas as pl
from jax.experimental.pallas import tpu as pltpu

F32 = jnp.float32
BF16 = jnp.bfloat16

D_MODEL = 1024
PAST_LEN = 16384
GLA_HEADS = 4
GLA_DK = 64
GLA_DV = 128
GLA_QK = GLA_HEADS * GLA_DK
GLA_WIDTH = GLA_HEADS * GLA_DV
GLA_GATE_RANK = 16
GLA_GATE_NORM = 16.0
GLA_CHUNK = 64
GLA_SUB = 16
HEAD_DIM = 64
SWA_HEADS = 8
SWA_KV_HEADS = 2
SWA_WIDTH = SWA_HEADS * HEAD_DIM
SWA_KV_WIDTH = SWA_KV_HEADS * HEAD_DIM
WINDOW = 128
ROPE_DIM = 16
ROPE_THETA = 500000.0
N_EXPERTS = 32
TOP_K = 4
SWIGLU_LIMIT = 7.0
SWIGLU_ALPHA = 1.702
NORM_EPS = 1e-5
DEEPNORM_ALPHA = 2.0 ** 0.25

LANES = 128
SUBLANES = 8
SAMPLE_PAD = 16
MOE_TM = 256
VMEM_LIMIT = 56 * 1024 * 1024

_C_QK = (0, 512)
_C_VG = (512, 1024)
_C_OG = (1024, 1536)
_C_QS = (1536, 2048)
_C_KV = (2048, 2304)
_C_GLR = (2304, 2432)
D_IN_PAD = 2432


def _cparams(sem):
    return pltpu.CompilerParams(dimension_semantics=sem, vmem_limit_bytes=VMEM_LIMIT)


def _proj_kernel(x_ref, w_ref, wgk_ref, bgk_ref, ca_ref, cb_ref, cc_ref,
                 qg_ref, kg_ref, vg_ref, og_ref, gk_ref, qs_ref, ks_ref, vs_ref):
    x = x_ref[...].astype(BF16)

    def mm(c):
        return jnp.dot(x, w_ref[:, c[0]:c[1]], preferred_element_type=F32)

    qk = mm(_C_QK)
    qg_ref[...] = qk[:, :GLA_QK]
    kg_ref[...] = qk[:, GLA_QK:]
    vg_ref[...] = mm(_C_VG)
    og_ref[...] = mm(_C_OG)
    qs = mm(_C_QS)
    kv = mm(_C_KV)
    glr = mm(_C_GLR)

    z = jnp.dot(glr.astype(BF16), wgk_ref[...], preferred_element_type=F32) + bgk_ref[...]
    gk_ref[...] = -(jnp.maximum(-z, 0.0) + jnp.log1p(jnp.exp(-jnp.abs(z)))) * (1.0 / GLA_GATE_NORM)

    ca, cb, cc = ca_ref[...], cb_ref[...], cc_ref[...]

    def rope(t):
        return t * ca + pltpu.roll(t, LANES - ROPE_DIM // 2, 1) * cb + pltpu.roll(t, ROPE_DIM // 2, 1) * cc

    for j in range(SWA_WIDTH // LANES):
        qs_ref[:, j * LANES:(j + 1) * LANES] = rope(qs[:, j * LANES:(j + 1) * LANES])
    ks_ref[...] = rope(kv[:, :SWA_KV_WIDTH])
    vs_ref[...] = kv[:, SWA_KV_WIDTH:]


def _proj(x, w_r, wgk, bgk, tabs, tm):
    t = x.shape[0]
    nb_tab = tabs[0].shape[0] // tm
    row = lambda i: (i, 0)
    const = lambda i: (0, 0)
    tab = lambda i: (i % nb_tab, 0)
    widths = (GLA_QK, GLA_QK, GLA_WIDTH, GLA_WIDTH, GLA_QK, SWA_WIDTH, SWA_KV_WIDTH, SWA_KV_WIDTH)
    return pl.pallas_call(
        _proj_kernel,
        grid=(t // tm,),
        in_specs=[pl.BlockSpec((tm, D_MODEL), row),
                  pl.BlockSpec((D_MODEL, D_IN_PAD), const),
                  pl.BlockSpec((LANES, GLA_QK), const),
                  pl.BlockSpec((1, GLA_QK), const),
                  pl.BlockSpec((tm, LANES), tab),
                  pl.BlockSpec((tm, LANES), tab),
                  pl.BlockSpec((tm, LANES), tab)],
        out_specs=[pl.BlockSpec((tm, w), row) for w in widths],
        out_shape=[jax.ShapeDtypeStruct((t, w), F32) for w in widths],
        compiler_params=_cparams(("parallel",)),
        name="proj",
    )(x, w_r, wgk, bgk, *tabs)


def _rope_tables(pos):
    half = ROPE_DIM // 2
    freqs = ROPE_THETA ** (-jnp.arange(half, dtype=F32) * 2.0 / ROPE_DIM)
    ang = pos.astype(F32)[:, None] * freqs[None, :]
    cos, sin = jnp.cos(ang), jnp.sin(ang)
    n = pos.shape[0]
    rest = HEAD_DIM - ROPE_DIM
    a = jnp.concatenate([cos, cos, jnp.ones((n, rest), F32)], axis=1)
    b = jnp.concatenate([-sin, jnp.zeros((n, half + rest), F32)], axis=1)
    c = jnp.concatenate([jnp.zeros((n, half), F32), sin, jnp.zeros((n, rest), F32)], axis=1)
    rep = LANES // HEAD_DIM
    return tuple(jnp.tile(t, (1, rep)) for t in (a, b, c))


def _split3(a):
    a1 = a.astype(BF16)
    r1 = a - a1.astype(F32)
    a2 = r1.astype(BF16)
    a3 = (r1 - a2.astype(F32)).astype(BF16)
    return a1, a2, a3


_NT = (((1,), (1,)), ((), ()))
_TN = (((0,), (0,)), ((), ()))


def _gla_chunk(q, k, g, v, og, s_pairs, ng, ind, c, valid):
    assert c in (GLA_SUB, GLA_DK)
    nsub = c // GLA_SUB
    row = lax.broadcasted_iota(jnp.int32, (c, 1), 0)
    if valid < c:
        live = row < valid
        g = jnp.where(live, g, 0.0)
        k = jnp.where(live, k, 0.0)
        v = jnp.where(live, v, 0.0)
    q = q * (GLA_DK ** -0.5)

    tri = jnp.where(lax.broadcasted_iota(jnp.int32, (c, c), 0) >= lax.broadcasted_iota(jnp.int32, (c, c), 1),
                    1.0, 0.0).astype(BF16)
    g3 = _split3(g)
    b = sum(jnp.dot(tri, gi, preferred_element_type=F32) for gi in g3)
    ones = jnp.ones((c, LANES), BF16)

    lane = lax.broadcasted_iota(jnp.int32, (1, LANES), 1)
    half = (lane < GLA_DK, lane >= GLA_DK)
    col = lane & (GLA_DK - 1)
    sub_of = lambda t: jnp.right_shift(t, 4)
    diag_keep = (sub_of(col) == sub_of(row)) & (col >= row) & (col < c)

    outs = []
    new_s = []
    for p in range(GLA_HEADS // 2):
        sl = slice(p * LANES, (p + 1) * LANES)
        qp, kp, bp = q[:, sl], k[:, sl], b[:, sl]
        s_p = s_pairs[p]

        def sub_rows(t, off):
            return jnp.concatenate(
                [jnp.broadcast_to(t[GLA_SUB * j + off:GLA_SUB * j + off + 1, :], (GLA_SUB, LANES))
                 for j in range(nsub)], axis=0)

        tiles = []
        for tl in range(GLA_SUB):
            w = kp * sub_rows(qp, tl) * jnp.exp(jnp.minimum(sub_rows(bp, tl) - bp, 0.0))
            tiles.append(w.astype(BF16))
        att_t = jnp.dot(jnp.concatenate(tiles, axis=1), ind, preferred_element_type=F32)
        att_t = jnp.where(diag_keep, att_t, 0.0)

        if nsub > 1:
            qt = qp * jnp.exp(jnp.minimum(bp - sub_rows(bp, 0), 0.0))
            kts = []
            for i in range(1, nsub):
                r_i = bp[GLA_SUB * i:GLA_SUB * i + 1, :]
                kts.append(jnp.where(row < GLA_SUB * i,
                                     kp * jnp.exp(jnp.minimum(r_i - bp, 0.0)), 0.0).astype(BF16))
            lhs = jnp.concatenate(kts, axis=1)
            rhs_rows = []
            for hh in range(2):
                qh = jnp.where(half[hh], qt, 0.0)
                rhs_rows.append(jnp.concatenate(
                    [jnp.where(sub_of(row) == i, qh, 0.0).astype(BF16) for i in range(1, nsub)], axis=1))
            rhs = jnp.concatenate(rhs_rows, axis=0)
            att_t = att_t + lax.dot_general(lhs, rhs, _NT, preferred_element_type=F32)

        att_b = att_t.astype(BF16)
        qe = qp * jnp.exp(bp)
        kd = kp * jnp.exp(bp[c - 1:c, :] - bp)
        upd = jnp.zeros((LANES, GLA_DV), F32)
        for hh in range(2):
            h = 2 * p + hh
            vh = v[:, h * GLA_DV:(h + 1) * GLA_DV].astype(BF16)
            z = lax.dot_general(att_b, vh, _TN, preferred_element_type=F32)
            o_intra = z[hh * GLA_DK:hh * GLA_DK + c, :]
            o_inter = jnp.dot(jnp.where(half[hh], qe, 0.0).astype(BF16), s_p.astype(BF16),
                              preferred_element_type=F32)
            o = o_inter + o_intra
            o = o * lax.rsqrt(jnp.mean(o * o, axis=-1, keepdims=True) + NORM_EPS) * ng
            gate = og[:, h * GLA_DV:(h + 1) * GLA_DV]
            outs.append(o * (gate * jax.nn.sigmoid(gate)))
            upd = upd + lax.dot_general(jnp.where(half[hh], kd, 0.0).astype(BF16), vh, _TN,
                                        preferred_element_type=F32)
        dcol = sum(lax.dot_general(gi[:, sl], ones, _TN, preferred_element_type=F32) for gi in g3)
        new_s.append(s_p * jnp.exp(dcol) + upd)
    return jnp.concatenate(outs, axis=1), new_s


def _gla_kernel(q_ref, k_ref, g_ref, v_ref, og_ref, s0_ref, ng_ref, ind_ref,
                o_ref, sn_ref, s_scr, *, sb, nc, c, valid, out_rows):
    tb = pl.program_id(1)

    @pl.when(tb == 0)
    def _():
        s_scr[...] = s0_ref[...]

    ng = ng_ref[...]
    ind = ind_ref[...]

    def seq_body(sq, carry):
        def chunk_body(ci, carry2):
            r0 = pl.multiple_of(ci * c, c)
            rows = pl.ds(r0, c)
            s_pairs = [s_scr[sq, p] for p in range(GLA_HEADS // 2)]
            o, new_s = _gla_chunk(q_ref[sq, rows, :], k_ref[sq, rows, :], g_ref[sq, rows, :],
                                  v_ref[sq, rows, :], og_ref[sq, rows, :], s_pairs, ng, ind, c, valid)
            for p in range(GLA_HEADS // 2):
                s_scr[sq, p] = new_s[p]
            if out_rows == c:
                o_ref[sq, rows, :] = o
            else:
                o_ref[sq, :, :] = o[:out_rows, :]
            return carry2

        return lax.fori_loop(0, nc, chunk_body, carry)

    lax.fori_loop(0, sb, seq_body, 0)

    @pl.when(tb == pl.num_programs(1) - 1)
    def _():
        sn_ref[...] = s_scr[...]


def _gla_indicator():
    k = np.arange(GLA_SUB * LANES)
    n = np.arange(LANES)
    tl, hf = k // LANES, (k % LANES) // GLA_DK
    m = (hf[:, None] == (n // GLA_DK)[None, :]) & ((n % GLA_SUB)[None, :] == tl[:, None])
    return jnp.asarray(m, dtype=BF16)


def _gla(q, k, g, v, og, s0, ng, *, c, valid, sb, nc, out_rows):
    ns, l, _ = q.shape
    tb = nc * c
    ob = out_rows if out_rows != c else tb
    ol = l if out_rows == c else out_rows
    seq = lambda i, j: (i, j, 0)
    st = lambda i, j: (i, 0, 0, 0)
    kern = functools.partial(_gla_kernel, sb=sb, nc=nc, c=c, valid=valid, out_rows=out_rows)
    return pl.pallas_call(
        kern,
        grid=(ns // sb, l // tb),
        in_specs=[pl.BlockSpec((sb, tb, GLA_QK), seq)] * 3
                 + [pl.BlockSpec((sb, tb, GLA_WIDTH), seq)] * 2
                 + [pl.BlockSpec((sb, 2, LANES, GLA_DV), st),
                    pl.BlockSpec((1, GLA_DV), lambda i, j: (0, 0)),
                    pl.BlockSpec((GLA_SUB * LANES, LANES), lambda i, j: (0, 0))],
        out_specs=[pl.BlockSpec((sb, ob, GLA_WIDTH), seq),
                   pl.BlockSpec((sb, 2, LANES, GLA_DV), st)],
        out_shape=[jax.ShapeDtypeStruct((ns, ol, GLA_WIDTH), F32),
                   jax.ShapeDtypeStruct((ns, 2, LANES, GLA_DV), F32)],
        scratch_shapes=[pltpu.VMEM((sb, 2, LANES, GLA_DV), F32)],
        compiler_params=_cparams(("parallel", "arbitrary")),
        name="gla",
    )(q, k, g, v, og, s0, ng, _gla_indicator())


def _place(t, lane_lo, src_half, dst_half):
    keep = lane_lo if src_half == 0 else jnp.logical_not(lane_lo)
    t = jnp.where(keep, t, 0.0)
    if src_half != dst_half:
        t = pltpu.roll(t, HEAD_DIM, 1)
    return t


def _swa_prompt_kernel(sink_ref, q_ref, kc_ref, kp_ref, vc_ref, vp_ref, o_ref):
    n = pl.program_id(1)
    kband = jnp.concatenate([kp_ref[0], kc_ref[0]], axis=0)
    vband = jnp.concatenate([vp_ref[0], vc_ref[0]], axis=0)
    lane_lo = lax.broadcasted_iota(jnp.int32, (1, LANES), 1) < HEAD_DIM
    qpos = n * WINDOW + lax.broadcasted_iota(jnp.int32, (WINDOW, 1), 0)
    kpos = (n - 1) * WINDOW + lax.broadcasted_iota(jnp.int32, (1, 2 * WINDOW), 1)
    dist = qpos - kpos
    mask = (dist >= 0) & (dist < WINDOW) & (kpos >= 0)
    for gi in range(SWA_KV_HEADS):
        kalt = [_place(kband, lane_lo, gi, hh).astype(BF16) for hh in range(2)]
        valt = [_place(vband, lane_lo, gi, hh).astype(BF16) for hh in range(2)]
        for pp in range(2):
            pidx = 2 * gi + pp
            qpair = q_ref[0, :, pidx * LANES:(pidx + 1) * LANES].astype(BF16)
            acc = jnp.zeros((WINDOW, LANES), F32)
            for hh in range(2):
                sink = sink_ref[2 * pidx + hh]
                s = lax.dot_general(qpair, kalt[hh], _NT, preferred_element_type=F32) * (HEAD_DIM ** -0.5)
                s = jnp.where(mask, s, -jnp.inf)
                m = jnp.maximum(jnp.max(s, axis=-1, keepdims=True), sink)
                p = jnp.exp(s - m)
                den = jnp.sum(p, axis=-1, keepdims=True) + jnp.exp(sink - m)
                acc = acc + jnp.dot(p.astype(BF16), valt[hh], preferred_element_type=F32) / den
            o_ref[0, :, pidx * LANES:(pidx + 1) * LANES] = acc


def _swa_prompt(q, k, v, sinks):
    b, l, _ = q.shape
    cur = lambda i, j, s: (i, j, 0)
    prev = lambda i, j, s: (i, jnp.maximum(j - 1, 0), 0)
    return pl.pallas_call(
        _swa_prompt_kernel,
        grid_spec=pltpu.PrefetchScalarGridSpec(
            num_scalar_prefetch=1,
            grid=(b, l // WINDOW),
            in_specs=[pl.BlockSpec((1, WINDOW, SWA_WIDTH), cur),
                      pl.BlockSpec((1, WINDOW, SWA_KV_WIDTH), cur),
                      pl.BlockSpec((1, WINDOW, SWA_KV_WIDTH), prev),
                      pl.BlockSpec((1, WINDOW, SWA_KV_WIDTH), cur),
                      pl.BlockSpec((1, WINDOW, SWA_KV_WIDTH), prev)],
            out_specs=pl.BlockSpec((1, WINDOW, SWA_WIDTH), cur)),
        out_shape=jax.ShapeDtypeStruct((b, l, SWA_WIDTH), F32),
        compiler_params=_cparams(("parallel", "parallel")),
        name="swa_prompt",
    )(sinks, q, k, k, v, v)


_QROWS = 8


def _swa_sample_kernel(sink_ref, q_ref, kn_ref, vn_ref, ck_ref, cv_ref, o_ref, *, bb, ntok):
    lane_lo = lax.broadcasted_iota(jnp.int32, (1, LANES), 1) < HEAD_DIM
    nrow = 4 * _QROWS
    rowi = lax.broadcasted_iota(jnp.int32, (nrow, 1), 0)
    t = rowi & (_QROWS - 1)
    blk_of_row = jnp.right_shift(rowi, 3)
    j = lax.broadcasted_iota(jnp.int32, (1, 2 * WINDOW), 1)
    jn = j - WINDOW
    mask = ((j < WINDOW) & (j > t)) | ((jn >= 0) & (jn <= t) & (jn < ntok))
    zpad = jnp.zeros((WINDOW - _QROWS, LANES), F32)

    def body(bi, carry):
        kk = jnp.concatenate([ck_ref[bi], kn_ref[bi, 0:_QROWS, :], zpad], axis=0)
        vv = jnp.concatenate([cv_ref[bi], vn_ref[bi, 0:_QROWS, :], zpad], axis=0)
        for gi in range(SWA_KV_HEADS):
            kboth = (_place(kk, lane_lo, gi, 0) + _place(kk, lane_lo, gi, 1)).astype(BF16)
            vboth = (_place(vv, lane_lo, gi, 0) + _place(vv, lane_lo, gi, 1)).astype(BF16)
            pieces = []
            sink_col = jnp.zeros((nrow, 1), F32)
            for pp in range(2):
                pidx = 2 * gi + pp
                qpair = q_ref[bi, 0:_QROWS, pidx * LANES:(pidx + 1) * LANES]
                for hh in range(2):
                    keep = lane_lo if hh == 0 else jnp.logical_not(lane_lo)
                    pieces.append(jnp.where(keep, qpair, 0.0))
                    blk = 2 * pp + hh
                    sink_col = jnp.where(blk_of_row == blk, sink_ref[2 * pidx + hh], sink_col)
            qst = jnp.concatenate(pieces, axis=0).astype(BF16)
            s = lax.dot_general(qst, kboth, _NT, preferred_element_type=F32) * (HEAD_DIM ** -0.5)
            s = jnp.where(mask, s, -jnp.inf)
            m = jnp.maximum(jnp.max(s, axis=-1, keepdims=True), sink_col)
            p = jnp.exp(s - m)
            den = jnp.sum(p, axis=-1, keepdims=True) + jnp.exp(sink_col - m)
            pv = jnp.dot(p.astype(BF16), vboth, preferred_element_type=F32) / den
            for pp in range(2):
                pidx = 2 * gi + pp
                r0 = 2 * pp * _QROWS
                o = jnp.where(lane_lo, pv[r0:r0 + _QROWS, :], pv[r0 + _QROWS:r0 + 2 * _QROWS, :])
                o_ref[bi, :, pidx * LANES:(pidx + 1) * LANES] = o[:ntok, :]
        return carry

    lax.fori_loop(0, bb, body, 0)


def _swa_sample(q, kn, vn, ck, cv, sinks, ntok, bb):
    b = q.shape[0]
    blk = lambda i, s: (i, 0, 0)
    return pl.pallas_call(
        functools.partial(_swa_sample_kernel, bb=bb, ntok=ntok),
        grid_spec=pltpu.PrefetchScalarGridSpec(
            num_scalar_prefetch=1,
            grid=(b // bb,),
            in_specs=[pl.BlockSpec((bb, SAMPLE_PAD, SWA_WIDTH), blk),
                      pl.BlockSpec((bb, SAMPLE_PAD, SWA_KV_WIDTH), blk),
                      pl.BlockSpec((bb, SAMPLE_PAD, SWA_KV_WIDTH), blk),
                      pl.BlockSpec((bb, WINDOW, SWA_KV_WIDTH), blk),
                      pl.BlockSpec((bb, WINDOW, SWA_KV_WIDTH), blk)],
            out_specs=pl.BlockSpec((bb, ntok, SWA_WIDTH), blk)),
        out_shape=jax.ShapeDtypeStruct((b, ntok, SWA_WIDTH), F32),
        compiler_params=_cparams(("parallel",)),
        name="swa_sample",
    )(sinks, q, kn, vn, ck, cv)


def _store_tok8(ref, val, n, lead=()):
    for s in range(SUBLANES):
        ref[lead + (pl.ds(s, n, stride=SUBLANES), slice(None))] = val[:, s * LANES:(s + 1) * LANES]


def _load_tok8(ref, n, lead=()):
    return jnp.concatenate(
        [ref[lead + (pl.ds(s, n, stride=SUBLANES), slice(None))] for s in range(SUBLANES)], axis=1)


def _layer_norm(y, g, b):
    mu = jnp.mean(y, axis=-1, keepdims=True)
    yc = y - mu
    var = jnp.mean(yc * yc, axis=-1, keepdims=True)
    return yc * lax.rsqrt(var + NORM_EPS) * g + b


def _mix_kernel(ogp_ref, osp_ref, xp_ref, ogs_ref, oss_ref, xs_ref, wo_ref, g_ref, b_ref,
                wrh_ref, wrl_ref, br_ref, x1_ref, ri_ref, rw_ref, cnt_ref, carry, *, nb_prompt, tm):
    i = pl.program_id(0)

    @pl.when(i == 0)
    def _():
        carry[...] = jnp.zeros_like(carry)

    is_s = i >= nb_prompt
    og = jnp.where(is_s, ogs_ref[...], ogp_ref[...]).astype(BF16)
    os_ = jnp.where(is_s, oss_ref[...], osp_ref[...]).astype(BF16)
    x = jnp.where(is_s, xs_ref[...], xp_ref[...])
    mix = (jnp.dot(og, wo_ref[0:GLA_WIDTH, :], preferred_element_type=F32)
           + jnp.dot(os_, wo_ref[GLA_WIDTH:, :], preferred_element_type=F32))
    x1 = _layer_norm(DEEPNORM_ALPHA * x + mix, g_ref[...], b_ref[...])
    _store_tok8(x1_ref, x1, tm)

    xh = x1.astype(BF16)
    xl = (x1 - xh.astype(F32)).astype(BF16)
    logits = (jnp.dot(xh, wrh_ref[...], preferred_element_type=F32)
              + jnp.dot(xh, wrl_ref[...], preferred_element_type=F32)
              + jnp.dot(xl, wrh_ref[...], preferred_element_type=F32)) + br_ref[...]

    lane = lax.broadcasted_iota(jnp.int32, (1, LANES), 1)
    lane_f = lane.astype(F32)
    vals = logits
    tops, idxs, hots = [], [], []
    for _ in range(TOP_K):
        m = jnp.max(vals, axis=-1, keepdims=True)
        idx = jnp.min(jnp.where(vals == m, lane_f, float(LANES)), axis=-1, keepdims=True)
        hot = lane_f == idx
        tops.append(m)
        idxs.append(idx)
        hots.append(hot)
        vals = jnp.where(hot, -jnp.inf, vals)
    es = [jnp.exp(tv - tops[0]) for tv in tops]
    den = es[0] + es[1] + es[2] + es[3]

    multi = sum(jnp.where(h, 1.0, 0.0) for h in hots)
    tri = jnp.where(lax.broadcasted_iota(jnp.int32, (tm, tm), 0) > lax.broadcasted_iota(jnp.int32, (tm, tm), 1),
                    1.0, 0.0).astype(BF16)
    cum = jnp.dot(tri, multi.astype(BF16), preferred_element_type=F32)
    base = cum + carry[...]
    ri = jnp.zeros((tm, LANES), jnp.int32)
    rw = jnp.zeros((tm, LANES), F32)
    for k in range(TOP_K):
        rank = jnp.sum(jnp.where(hots[k], base, 0.0), axis=-1, keepdims=True).astype(jnp.int32)
        ri = jnp.where(lane == k, idxs[k].astype(jnp.int32), ri)
        ri = jnp.where(lane == TOP_K + k, rank, ri)
        rw = jnp.where(lane == k, es[k] / den, rw)
    ri_ref[...] = ri
    rw_ref[...] = rw
    carry[...] = carry[...] + jnp.sum(multi, axis=0, keepdims=True)
    cnt_ref[...] = carry[...]


def _mix(og_p, os_p, x_p, og_s, os_s, x_s, wo, g, b, wrh, wrl, br, tm):
    tp, ts = x_p.shape[0], x_s.shape[0]
    nbp = tp // tm
    t_all = tp + ts
    prow = lambda i: (jnp.minimum(i, nbp - 1), 0)
    srow = lambda i: (jnp.maximum(i - nbp, 0), 0)
    row = lambda i: (i, 0)
    const = lambda i: (0, 0)
    return pl.pallas_call(
        functools.partial(_mix_kernel, nb_prompt=nbp, tm=tm),
        grid=(t_all // tm,),
        in_specs=[pl.BlockSpec((tm, GLA_WIDTH), prow), pl.BlockSpec((tm, SWA_WIDTH), prow),
                  pl.BlockSpec((tm, D_MODEL), prow),
                  pl.BlockSpec((tm, GLA_WIDTH), srow), pl.BlockSpec((tm, SWA_WIDTH), srow),
                  pl.BlockSpec((tm, D_MODEL), srow),
                  pl.BlockSpec((D_MODEL, D_MODEL), const),
                  pl.BlockSpec((1, D_MODEL), const), pl.BlockSpec((1, D_MODEL), const),
                  pl.BlockSpec((D_MODEL, LANES), const), pl.BlockSpec((D_MODEL, LANES), const),
                  pl.BlockSpec((1, LANES), const)],
        out_specs=[pl.BlockSpec((tm * SUBLANES, LANES), row), pl.BlockSpec((tm, LANES), row),
                   pl.BlockSpec((tm, LANES), row), pl.BlockSpec((1, LANES), const)],
        out_shape=[jax.ShapeDtypeStruct((t_all * SUBLANES, LANES), F32),
                   jax.ShapeDtypeStruct((t_all, LANES), jnp.int32),
                   jax.ShapeDtypeStruct((t_all, LANES), F32),
                   jax.ShapeDtypeStruct((1, LANES), F32)],
        scratch_shapes=[pltpu.VMEM((1, LANES), F32)],
        compiler_params=_cparams(("arbitrary",)),
        name="mix",
    )(og_p, os_p, x_p, og_s, os_s, x_s, wo, g, b, wrh, wrl, br)


_MOE_GROUPS = 4


def _moe_kernel(be_ref, na_ref, tok_cur, tok_nxt, tok_nx2, pos_prv, pos_cur, x_hbm, wg_ref, wu_ref, wd_ref,
                bg_ref, bu_ref, bd_ref, y_hbm, xbuf, ybuf, wbf, gsem, ssem, *, tm):
    j = pl.program_id(0)
    n_act = na_ref[0]
    slot = j % 2
    gslot = j % 3
    gslot1 = (j + 1) % 3
    gslot2 = (j + 2) % 3
    rows = tm * SUBLANES

    def copy_in(tok_ref, r, s):
        src = x_hbm.at[pl.ds(pl.multiple_of(tok_ref[0, 0, r], SUBLANES), SUBLANES), :]
        return pltpu.make_async_copy(src, xbuf.at[s, pl.ds(r * SUBLANES, SUBLANES), :], gsem.at[s])

    def copy_out(pos_ref, r, s):
        dst = y_hbm.at[pl.ds(pl.multiple_of(pos_ref[0, 0, r], SUBLANES), SUBLANES), :]
        return pltpu.make_async_copy(ybuf.at[s, pl.ds(r * SUBLANES, SUBLANES), :], dst, ssem.at[s])

    def wait_in(s):
        pltpu.make_async_copy(x_hbm.at[pl.ds(0, rows), :], xbuf.at[s], gsem.at[s]).wait()

    def wait_out(s):
        pltpu.make_async_copy(ybuf.at[s], y_hbm.at[pl.ds(0, rows), :], ssem.at[s]).wait()

    @pl.when(j == 0)
    def _():
        def start(r, c):
            copy_in(tok_cur, r, 0).start()
            copy_in(tok_nxt, r, 1).start()
            return c
        lax.fori_loop(0, tm, start, 0, unroll=8)
        ybuf[...] = jnp.zeros_like(ybuf)

    @pl.when(j < n_act)
    def _():
        changed = jnp.logical_or(j == 0, be_ref[j] != be_ref[jnp.maximum(j - 1, 0)])

        @pl.when(changed)
        def _():
            wbf[0] = wg_ref[0].astype(BF16)
            wbf[1] = wu_ref[0].astype(BF16)
            wbf[2] = wd_ref[0].astype(BF16)

        wait_in(gslot)
        x = _load_tok8(xbuf, tm, (gslot,)).astype(BF16)
        y = jnp.zeros((tm, D_MODEL), F32)
        per = tm // _MOE_GROUPS
        width = D_MODEL // _MOE_GROUPS
        for grp in range(_MOE_GROUPS):
            for r in range(grp * per, (grp + 1) * per):
                copy_in(tok_nx2, r, gslot2).start(priority=r % 2)
                copy_out(pos_prv, r, 1 - slot).start(priority=(r + 1) % 2)
            cs = slice(grp * width, (grp + 1) * width)
            gate = jnp.dot(x, wbf[0, :, cs], preferred_element_type=F32) + bg_ref[0, :, cs]
            up = jnp.dot(x, wbf[1, :, cs], preferred_element_type=F32) + bu_ref[0, :, cs]
            gate = jnp.minimum(gate, SWIGLU_LIMIT)
            up = jnp.clip(up, -SWIGLU_LIMIT, SWIGLU_LIMIT)
            h = gate * jax.nn.sigmoid(SWIGLU_ALPHA * gate) * (up + 1.0)
            y = y + jnp.dot(h.astype(BF16), wbf[2, cs, :], preferred_element_type=F32)
        y = y + bd_ref[0]

        @pl.when(j >= 1)
        def _():
            wait_out(slot)

        _store_tok8(ybuf, y, tm, (slot,))

        @pl.when(j == n_act - 1)
        def _():
            def start(r, c):
                copy_out(pos_cur, r, slot).start()
                return c
            lax.fori_loop(0, tm, start, 0, unroll=8)
            wait_out(slot)
            wait_out(1 - slot)
            wait_in(gslot1)
            wait_in(gslot2)


def _moe(x1, tok8, pos8, blk_expert, n_active, wg, wu, wd, bg, bu, bd, tm):
    t_all = x1.shape[0] // SUBLANES
    nslot = tok8.shape[0]
    nbm = nslot // tm
    n_out = t_all * TOP_K + tm
    dump8 = (t_all * TOP_K + jnp.arange(tm, dtype=jnp.int32)) * SUBLANES
    tok3 = tok8.reshape(nbm, 1, tm)
    pos3 = jnp.concatenate([dump8, pos8]).reshape(nbm + 1, 1, tm)
    cur = lambda j, be, na: (j, 0, 0)
    nxt = lambda j, be, na: (jnp.minimum(j + 1, nbm - 1), 0, 0)
    nx2 = lambda j, be, na: (jnp.minimum(j + 2, nbm - 1), 0, 0)
    shifted = lambda j, be, na: (j + 1, 0, 0)
    wmap = lambda j, be, na: (be[j], 0, 0)
    smem = functools.partial(pl.BlockSpec, memory_space=pltpu.SMEM)
    return pl.pallas_call(
        functools.partial(_moe_kernel, tm=tm),
        grid_spec=pltpu.PrefetchScalarGridSpec(
            num_scalar_prefetch=2,
            grid=(nbm,),
            in_specs=[smem((1, 1, tm), cur), smem((1, 1, tm), nxt), smem((1, 1, tm), nx2),
                      smem((1, 1, tm), cur), smem((1, 1, tm), shifted),
                      pl.BlockSpec(memory_space=pl.ANY),
                      pl.BlockSpec((1, D_MODEL, D_MODEL), wmap),
                      pl.BlockSpec((1, D_MODEL, D_MODEL), wmap),
                      pl.BlockSpec((1, D_MODEL, D_MODEL), wmap),
                      pl.BlockSpec((1, 1, D_MODEL), wmap),
                      pl.BlockSpec((1, 1, D_MODEL), wmap),
                      pl.BlockSpec((1, 1, D_MODEL), wmap)],
            out_specs=pl.BlockSpec(memory_space=pl.ANY),
            scratch_shapes=[pltpu.VMEM((3, tm * SUBLANES, LANES), F32),
                            pltpu.VMEM((2, tm * SUBLANES, LANES), F32),
                            pltpu.VMEM((3, D_MODEL, D_MODEL), BF16),
                            pltpu.SemaphoreType.DMA((3,)),
                            pltpu.SemaphoreType.DMA((2,))]),
        out_shape=jax.ShapeDtypeStruct((n_out * SUBLANES, LANES), F32),
        compiler_params=_cparams(("arbitrary",)),
        name="moe",
    )(blk_expert, n_active, tok3, tok3, tok3, pos3, pos3, x1, wg, wu, wd,
      bg.reshape(N_EXPERTS, 1, D_MODEL), bu.reshape(N_EXPERTS, 1, D_MODEL),
      bd.reshape(N_EXPERTS, 1, D_MODEL))


def _combine_kernel(y0_ref, y1_ref, y2_ref, y3_ref, rw_ref, x1_ref, g_ref, b_ref, o_ref, *, tm):
    rw = rw_ref[...]
    ffn = rw[:, 0:1] * _load_tok8(y0_ref, tm)
    for k, y_ref in enumerate((y1_ref, y2_ref, y3_ref), start=1):
        ffn = ffn + rw[:, k:k + 1] * _load_tok8(y_ref, tm)
    o_ref[...] = _layer_norm(DEEPNORM_ALPHA * _load_tok8(x1_ref, tm) + ffn, g_ref[...], b_ref[...])


def _combine(y, rw, x1, g, b, row0, nrows, tm):
    t_all = x1.shape[0] // SUBLANES
    off = row0 // tm
    nbt = t_all // tm
    row = lambda i: (i + off, 0)
    const = lambda i: (0, 0)
    yk = lambda k: pl.BlockSpec((tm * SUBLANES, LANES), lambda i: (k * nbt + i + off, 0))
    return pl.pallas_call(
        functools.partial(_combine_kernel, tm=tm),
        grid=(nrows // tm,),
        in_specs=[yk(0), yk(1), yk(2), yk(3), pl.BlockSpec((tm, LANES), row),
                  pl.BlockSpec((tm * SUBLANES, LANES), row),
                  pl.BlockSpec((1, D_MODEL), const), pl.BlockSpec((1, D_MODEL), const)],
        out_specs=pl.BlockSpec((tm, D_MODEL), lambda i: (i, 0)),
        out_shape=jax.ShapeDtypeStruct((nrows, D_MODEL), F32),
        compiler_params=_cparams(("parallel",)),
        name="combine",
    )(y, y, y, y, rw, x1, g, b)


def _route_tables(ri, counts_f, t_all, tm):
    a = t_all * TOP_K
    nbm = a // tm + N_EXPERTS
    nslot = nbm * tm
    counts = counts_f[0, :N_EXPERTS].astype(jnp.int32)
    padded = (counts + tm - 1) // tm * tm
    pad_end = jnp.cumsum(padded)
    pad_start = pad_end - padded
    idx = ri[:, 0:TOP_K]
    rank = ri[:, TOP_K:2 * TOP_K]
    dest = (pad_start[idx] + rank).reshape(-1)
    flat = jnp.arange(a, dtype=jnp.int32)
    inv = jnp.full((nslot,), -1, jnp.int32).at[dest].set(flat, unique_indices=True)
    is_pad = inv < 0
    tok = inv // TOP_K
    out_row = jnp.where(is_pad, a + jnp.arange(nslot, dtype=jnp.int32) % tm, (inv % TOP_K) * t_all + tok)
    tok8 = jnp.where(is_pad, 0, tok) * SUBLANES
    blk_row0 = jnp.arange(nbm, dtype=jnp.int32) * tm
    blk_expert = jnp.minimum(
        jnp.sum((pad_end[None, :] <= blk_row0[:, None]).astype(jnp.int32), axis=1), N_EXPERTS - 1)
    n_active = (pad_end[-1:] // tm).astype(jnp.int32)
    return tok8, out_row * SUBLANES, blk_expert, n_active


def kernel(x_prompt, x_sample, state_gla, cache_swa_k, cache_swa_v, w_in, w_gk_up, b_gk_up, gla_norm_g, swa_sinks, w_out, ln1_g, ln1_b, w_router, b_router, w_gate, b_gate, w_up, b_up, w_down, b_down, ln2_g, ln2_b):
    bp, lp, _ = x_prompt.shape
    bs, ls, _ = x_sample.shape
    tp, ts = bp * lp, bs * ls
    t_all = tp + ts
    tm = 512

    w = w_in[0]
    n_main = GLA_QK * 2 + GLA_WIDTH * 2
    w_r = jnp.concatenate([w[:, :n_main], w[:, n_main + GLA_GATE_RANK:], w[:, n_main:n_main + GLA_GATE_RANK],
                           jnp.zeros((D_MODEL, LANES - GLA_GATE_RANK), F32)], axis=1).astype(BF16)
    wgk = jnp.concatenate([w_gk_up[0], jnp.zeros((LANES - GLA_GATE_RANK, GLA_QK), F32)], axis=0).astype(BF16)
    bgk = b_gk_up[0].reshape(1, GLA_QK)
    ng = gla_norm_g[0].reshape(1, GLA_DV)
    sinks = swa_sinks[0]
    wo = w_out[0].astype(BF16)
    wr = jnp.concatenate([w_router[0], jnp.zeros((D_MODEL, LANES - N_EXPERTS), F32)], axis=1)
    wrh = wr.astype(BF16)
    wrl = (wr - wrh.astype(F32)).astype(BF16)
    br = jnp.concatenate([b_router[0], jnp.full((LANES - N_EXPERTS,), -1e30, F32)]).reshape(1, LANES)

    xp = x_prompt.reshape(tp, D_MODEL)
    tabs_p = _rope_tables(jnp.arange(lp))
    qg, kg, vg, og, gk, qs, ks, vs = _proj(xp, w_r, wgk, bgk, tabs_p, tm)
    r3 = lambda t: t.reshape(bp, lp, t.shape[-1])
    s0p = jnp.zeros((bp, 2, LANES, GLA_DV), F32)
    ogla_p, sn_p = _gla(r3(qg), r3(kg), r3(gk), r3(vg), r3(og), s0p, ng,
                        c=GLA_CHUNK, valid=GLA_CHUNK, sb=1, nc=8, out_rows=GLA_CHUNK)
    oswa_p = _swa_prompt(r3(qs), r3(ks), r3(vs), sinks)
    k_keep_p = ks.reshape(bp, lp, SWA_KV_HEADS, HEAD_DIM)[:, lp - WINDOW:]
    v_keep_p = vs.reshape(bp, lp, SWA_KV_HEADS, HEAD_DIM)[:, lp - WINDOW:]

    xs_pad = jnp.pad(x_sample, ((0, 0), (0, SAMPLE_PAD - ls), (0, 0))).reshape(bs * SAMPLE_PAD, D_MODEL)
    pos_s = PAST_LEN + jnp.arange(SAMPLE_PAD)
    tabs_s = tuple(jnp.tile(t, (tm // SAMPLE_PAD, 1)) for t in _rope_tables(pos_s))
    qg2, kg2, vg2, og2, gk2, qs2, ks2, vs2 = _proj(xs_pad, w_r, wgk, bgk, tabs_s, tm)
    r3s = lambda t: t.reshape(bs, SAMPLE_PAD, t.shape[-1])
    s0s = state_gla[0].reshape(bs, 2, LANES, GLA_DV)
    ogla_s, sn_s = _gla(r3s(qg2), r3s(kg2), r3s(gk2), r3s(vg2), r3s(og2), s0s, ng,
                        c=SAMPLE_PAD, valid=ls, sb=8, nc=1, out_rows=ls)
    ck = cache_swa_k[0].reshape(bs, WINDOW, SWA_KV_WIDTH)
    cv = cache_swa_v[0].reshape(bs, WINDOW, SWA_KV_WIDTH)
    oswa_s = _swa_sample(r3s(qs2), r3s(ks2), r3s(vs2), ck, cv, sinks, ls, 8)
    k_new = r3s(ks2)[:, :ls].reshape(bs, ls, SWA_KV_HEADS, HEAD_DIM)
    v_new = r3s(vs2)[:, :ls].reshape(bs, ls, SWA_KV_HEADS, HEAD_DIM)
    k_keep_s = jnp.concatenate([cache_swa_k[0][:, ls:], k_new], axis=1)
    v_keep_s = jnp.concatenate([cache_swa_v[0][:, ls:], v_new], axis=1)

    x1, ri, rw, cnt = _mix(ogla_p.reshape(tp, GLA_WIDTH), oswa_p.reshape(tp, SWA_WIDTH), xp,
                           ogla_s.reshape(ts, GLA_WIDTH), oswa_s.reshape(ts, SWA_WIDTH),
                           x_sample.reshape(ts, D_MODEL), wo,
                           ln1_g[0].reshape(1, D_MODEL), ln1_b[0].reshape(1, D_MODEL), wrh, wrl, br, tm)
    tok8, pos8, blk_expert, n_active = _route_tables(ri, cnt, t_all, MOE_TM)
    y = _moe(x1, tok8, pos8, blk_expert, n_active, w_gate[0], w_up[0], w_down[0],
             b_gate[0], b_up[0], b_down[0], MOE_TM)
    g2, b2 = ln2_g[0].reshape(1, D_MODEL), ln2_b[0].reshape(1, D_MODEL)
    y_p = _combine(y, rw, x1, g2, b2, 0, tp, tm)
    y_s = _combine(y, rw, x1, g2, b2, tp, ts, tm)

    return (y_p.reshape(bp, lp, D_MODEL),
            y_s.reshape(bs, ls, D_MODEL),
            sn_p.reshape(1, bp, GLA_HEADS, GLA_DK, GLA_DV),
            sn_s.reshape(1, bs, GLA_HEADS, GLA_DK, GLA_DV),
            k_keep_p[None], v_keep_p[None], k_keep_s[None], v_keep_s[None])
```
